```python
import math
import jax, jax.numpy as jnp
from jax import lax
import numpy as np

D_MODEL = 2048
BATCH = 4
SEQ = 4096
DEPTH = 2

GRID_W = 64
CTX_LEN = 256
MIX_WIDTH = D_MODEL
A_WIDTH = MIX_WIDTH // 4
A_HEADS = 4
A_HEAD_DIM = A_WIDTH // A_HEADS
CHUNK = 128
B_WIDTH = MIX_WIDTH // 4
CONV_WIDTH = 31
C_WIDTH = MIX_WIDTH - A_WIDTH - B_WIDTH
HEAD_DIM = 128
N_HEADS = C_WIDTH // HEAD_DIM
N_KV_HEADS = N_HEADS // 4
KV_GROUP = N_HEADS // N_KV_HEADS
KV_WIDTH = N_KV_HEADS * HEAD_DIM
Q_BLOCK = 128
ROPE_AXIS_DIM = HEAD_DIM // 2
ROPE_FREQS = ROPE_AXIS_DIM // 2
ROPE_THETA = 10000.0
ATTN_SCALE = HEAD_DIM ** -0.5
D_FF = 4 * D_MODEL
N_MOD = 6
LN_EPS = 1e-6
RMS_EPS = 1e-6
DEEPNORM_ALPHA = (2 * DEPTH) ** 0.25
DEEPNORM_BETA = (8 * DEPTH) ** -0.25
A_U_OFF = 0
A_V_OFF = A_U_OFF + A_WIDTH
B_VAL_OFF = A_V_OFF + A_WIDTH
B_GATE_OFF = B_VAL_OFF + B_WIDTH
Q_OFF = B_GATE_OFF + B_WIDTH
K_OFF = Q_OFF + C_WIDTH
V_OFF = K_OFF + KV_WIDTH
IN_COLS = V_OFF + KV_WIDTH

kernel_name = "hymba_style_gmlp_conformer_gqa_dit_block"


def layer_norm(x, gain=None, bias=None):
    xf = x.astype(jnp.float32)
    mu = jnp.mean(xf, axis=-1, keepdims=True)
    var = jnp.mean(jnp.square(xf - mu), axis=-1, keepdims=True)
    y = (xf - mu) * lax.rsqrt(var + LN_EPS)
    if gain is not None:
        y = y * gain.astype(jnp.float32) + bias.astype(jnp.float32)
    return y.astype(x.dtype)


def rms_norm(x, gain):
    xf = x.astype(jnp.float32)
    y = xf * lax.rsqrt(jnp.mean(jnp.square(xf), axis=-1, keepdims=True) + RMS_EPS)
    return (y * gain.astype(jnp.float32)).astype(x.dtype)


def modulate(x, shift, scale):
    return layer_norm(x) * (1 + scale) + shift


def rope_tables(rows):
    n_tok = rows * GRID_W
    row = jnp.broadcast_to(jnp.arange(rows, dtype=jnp.float32)[:, None], (rows, GRID_W)).reshape(n_tok)
    col = jnp.broadcast_to(jnp.arange(GRID_W, dtype=jnp.float32)[None, :], (rows, GRID_W)).reshape(n_tok)
    freqs = ROPE_THETA ** (-jnp.arange(ROPE_FREQS, dtype=jnp.float32) / ROPE_FREQS)
    ang = jnp.stack([row[:, None] * freqs, col[:, None] * freqs], axis=1)
    return jnp.cos(ang), jnp.sin(ang)


def apply_axial_rope(t, cos, sin):
    tf = t.astype(jnp.float32).reshape(*t.shape[:-1], 2, 2, ROPE_FREQS)
    x1, x2 = tf[..., 0, :], tf[..., 1, :]
    c = cos[None, :, None]
    s = sin[None, :, None]
    out = jnp.stack([x1 * c - x2 * s, x1 * s + x2 * c], axis=-2)
    return out.reshape(t.shape).astype(t.dtype)


def chunk_mlp(p_u, p_v, ln_g, ln_b, w_s, b_s):
    u = jax.nn.gelu(p_u)
    v = layer_norm(jax.nn.gelu(p_v), ln_g, ln_b)
    bsz, n_tok, _ = v.shape
    v = v.reshape(bsz, n_tok // CHUNK, CHUNK, A_HEADS, A_HEAD_DIM)
    mixed = jnp.einsum('hpq,bnqhd->bnphd', w_s, v) + b_s.T[:, :, None]
    return u * mixed.reshape(bsz, n_tok, A_WIDTH)


def conv_module(p_val, p_gate, w_dw, b_dw, ln_g, ln_b):
    z = p_val * jax.nn.sigmoid(p_gate)
    half = CONV_WIDTH // 2
    z = lax.conv_general_dilated(
        z, w_dw[:, None, :].astype(z.dtype), window_strides=(1,), padding=[(half, half)],
        dimension_numbers=('NWC', 'WIO', 'NWC'), feature_group_count=B_WIDTH) + b_dw
    return jax.nn.silu(layer_norm(z, ln_g, ln_b))


def split_kv(p_kv, k_gain):
    bsz, n_tok, _ = p_kv.shape
    k = rms_norm(p_kv[..., :KV_WIDTH].reshape(bsz, n_tok, N_KV_HEADS, HEAD_DIM), k_gain)
    v = p_kv[..., KV_WIDTH:].reshape(bsz, n_tok, N_KV_HEADS, HEAD_DIM)
    return k, v


def attend(q_blk, k, v):
    s = jnp.einsum('bqkgd,bskd->bkgqs', q_blk, k, preferred_element_type=jnp.float32) * ATTN_SCALE
    p = jax.nn.softmax(s, axis=-1).astype(v.dtype)
    return jnp.einsum('bkgqs,bskd->bqkgd', p, v)


def mix_tokens(p, k_ctx, v_ctx, a_ln_g, a_ln_b, a_ws, a_bs, b_dw, b_dw_bias, b_ln_g, b_ln_b,
               q_gain, k_gain, w_out, rope):
    bsz, n_tok, _ = p.shape
    out_a = chunk_mlp(p[..., A_U_OFF:A_V_OFF], p[..., A_V_OFF:B_VAL_OFF], a_ln_g, a_ln_b, a_ws, a_bs)
    out_b = conv_module(p[..., B_VAL_OFF:B_GATE_OFF], p[..., B_GATE_OFF:Q_OFF], b_dw, b_dw_bias, b_ln_g, b_ln_b)
    q = rms_norm(p[..., Q_OFF:K_OFF].reshape(bsz, n_tok, N_HEADS, HEAD_DIM), q_gain)
    if rope is None:
        q = q.reshape(bsz, n_tok, N_KV_HEADS, KV_GROUP, HEAD_DIM)
        out_c = attend(q, k_ctx, v_ctx).reshape(bsz, n_tok, C_WIDTH)
    else:
        cos, sin = rope
        q = apply_axial_rope(q, cos, sin)
        k_lat, v_lat = split_kv(p[..., K_OFF:], k_gain)
        k_lat = apply_axial_rope(k_lat, cos, sin)
        k_all = jnp.concatenate([k_ctx, k_lat], axis=1)
        v_all = jnp.concatenate([v_ctx, v_lat], axis=1)
        n_blk = n_tok // Q_BLOCK
        q_blocks = q.reshape(bsz, n_blk, Q_BLOCK, N_KV_HEADS, KV_GROUP, HEAD_DIM).transpose(1, 0, 2, 3, 4, 5)
        o = lax.map(lambda qb: attend(qb, k_all, v_all), q_blocks)
        out_c = o.transpose(1, 0, 2, 3, 4, 5).reshape(bsz, n_tok, C_WIDTH)
    return jnp.concatenate([out_a, out_b, out_c], axis=-1) @ w_out


def sq_relu_mlp(h, w_ff1, w_ff2):
    return jnp.square(jax.nn.relu(h @ w_ff1)) @ w_ff2


def setup_inputs(seed: int = 0) -> dict:
    key = jax.random.key(seed)
    ks = jax.random.split(key, 24)
    nrm = jax.random.normal
    f32 = jnp.float32
    D = D_MODEL
    return {
        "x": nrm(ks[0], (BATCH, SEQ, D), f32),
        "c": nrm(ks[1], (BATCH, D), f32),
        "ctx": nrm(ks[2], (BATCH, CTX_LEN, D), f32),
        "c_ctx": nrm(ks[3], (D,), f32),
        "w_mod": nrm(ks[4], (DEPTH, D, N_MOD * D), f32) * (0.5 * D ** -0.5),
        "b_mod": nrm(ks[5], (DEPTH, N_MOD * D), f32) * 0.01,
        "w_in": nrm(ks[6], (DEPTH, D, IN_COLS), f32) * D ** -0.5,
        "a_ln_g": 1.0 + 0.05 * nrm(ks[7], (DEPTH, A_WIDTH), f32),
        "a_ln_b": 0.02 * nrm(ks[8], (DEPTH, A_WIDTH), f32),
        "a_ws": nrm(ks[9], (DEPTH, A_HEADS, CHUNK, CHUNK), f32) * CHUNK ** -0.5,
        "a_bs": 1.0 + 0.05 * nrm(ks[10], (DEPTH, A_HEADS, CHUNK), f32),
        "b_dw": nrm(ks[11], (DEPTH, CONV_WIDTH, B_WIDTH), f32) * CONV_WIDTH ** -0.5,
        "b_dw_bias": 0.02 * nrm(ks[12], (DEPTH, B_WIDTH), f32),
        "b_ln_g": 1.0 + 0.05 * nrm(ks[13], (DEPTH, B_WIDTH), f32),
        "b_ln_b": 0.02 * nrm(ks[14], (DEPTH, B_WIDTH), f32),
        "q_gain": 1.0 + 0.05 * nrm(ks[15], (DEPTH, HEAD_DIM), f32),
        "k_gain": 1.0 + 0.05 * nrm(ks[16], (DEPTH, HEAD_DIM), f32),
        "w_out": nrm(ks[17], (DEPTH, MIX_WIDTH, D), f32) * (MIX_WIDTH ** -0.5 * DEEPNORM_BETA),
        "ln1_g": 1.0 + 0.05 * nrm(ks[18], (DEPTH, D), f32),
        "ln1_b": 0.02 * nrm(ks[19], (DEPTH, D), f32),
        "w_ff1": nrm(ks[20], (DEPTH, D, D_FF), f32) * D ** -0.5,
        "w_ff2": nrm(ks[21], (DEPTH, D_FF, D), f32) * (D_FF ** -0.5 * DEEPNORM_BETA),
        "ln2_g": 1.0 + 0.05 * nrm(ks[22], (DEPTH, D), f32),
        "ln2_b": 0.02 * nrm(ks[23], (DEPTH, D), f32),
    }


def reference(x, c, ctx, c_ctx, w_mod, b_mod, w_in, a_ln_g, a_ln_b, a_ws, a_bs, b_dw, b_dw_bias,
              b_ln_g, b_ln_b, q_gain, k_gain, w_out, ln1_g, ln1_b, w_ff1, w_ff2, ln2_g, ln2_b):
    bsz, n_tok, d = x.shape
    rows = n_tok // GRID_W
    rope = rope_tables(rows)
    cond_lat = jax.nn.silu(c)
    cond_ctx = jax.nn.silu(c_ctx)
    for l in range(DEPTH):
        last = l == DEPTH - 1
        m_lat = (cond_lat @ w_mod[l] + b_mod[l]).reshape(bsz, N_MOD, 1, d)
        m_ctx = (cond_ctx @ w_mod[l] + b_mod[l]).reshape(N_MOD, d)
        h = modulate(x, m_lat[:, 0], m_lat[:, 1])
        hc = modulate(ctx, m_ctx[0], m_ctx[1])
        if last:
            k_ctx, v_ctx = split_kv(hc @ w_in[l][:, K_OFF:], k_gain[l])
        else:
            pc = hc @ w_in[l]
            k_ctx, v_ctx = split_kv(pc[..., K_OFF:], k_gain[l])
        mix_params = (a_ln_g[l], a_ln_b[l], a_ws[l], a_bs[l], b_dw[l], b_dw_bias[l], b_ln_g[l], b_ln_b[l],
                      q_gain[l], k_gain[l], w_out[l])
        y = mix_tokens(h @ w_in[l], k_ctx, v_ctx, *mix_params, rope)
        x = layer_norm(DEEPNORM_ALPHA * x + m_lat[:, 2] * y, ln1_g[l], ln1_b[l])
        if not last:
            yc = mix_tokens(pc, k_ctx, v_ctx, *mix_params, None)
            ctx = layer_norm(DEEPNORM_ALPHA * ctx + m_ctx[2] * yc, ln1_g[l], ln1_b[l])
        f = sq_relu_mlp(modulate(x, m_lat[:, 3], m_lat[:, 4]), w_ff1[l], w_ff2[l])
        x = layer_norm(DEEPNORM_ALPHA * x + m_lat[:, 5] * f, ln2_g[l], ln2_b[l])
        if not last:
            fc = sq_relu_mlp(modulate(ctx, m_ctx[3], m_ctx[4]), w_ff1[l], w_ff2[l])
            ctx = layer_norm(DEEPNORM_ALPHA * ctx + m_ctx[5] * fc, ln2_g[l], ln2_b[l])
    return x
```

```python
import functools

import jax
import jax.numpy as jnp
from jax import lax
from jax.experimental import pallas as pl
from jax.experimental.pallas import tpu as pltpu

HEAD_DIM = 128
CHUNK = 128
A_HEADS = 4
A_WIDTH = A_HEADS * HEAD_DIM
B_WIDTH = 512
CONV_WIDTH = 31
CONV_HALF = CONV_WIDTH // 2
N_HEADS = 8
N_KV_HEADS = 2
KV_GROUP = N_HEADS // N_KV_HEADS
C_WIDTH = N_HEADS * HEAD_DIM
KV_WIDTH = N_KV_HEADS * HEAD_DIM
GROUP_Q_WIDTH = KV_GROUP * HEAD_DIM
GRID_W = 64
ROPE_FREQS = 32
ROPE_THETA = 10000.0
ATTN_SCALE = HEAD_DIM ** -0.5
N_MOD = 6
LN_EPS = 1e-6
RMS_EPS = 1e-6
AB_OFF = 0
B_OFF = 2 * A_WIDTH
Q_OFF = B_OFF + 2 * B_WIDTH
K_OFF = Q_OFF + C_WIDTH
IN_COLS = K_OFF + 2 * KV_WIDTH

MOD_ROWS = 8
MOD_TN = 512
ROW_TILE = 256
CONV_HALO = 16
ATTN_TQ = 256
ATTN_TK = 512
OUT_TM = 512
FF_TM = 512
FF_TF = 512
VMEM_LIMIT = 56 * 1024 * 1024

BF16 = jnp.bfloat16
F32 = jnp.float32


def _params(*sem):
    return pltpu.CompilerParams(dimension_semantics=sem, vmem_limit_bytes=VMEM_LIMIT)


def _ln(x):
    mu = jnp.mean(x, axis=-1, keepdims=True)
    xc = x - mu
    var = jnp.mean(xc * xc, axis=-1, keepdims=True)
    return xc * lax.rsqrt(var + LN_EPS)


def _mod_kernel(cond_ref, w_ref, b_ref, o_ref):
    a = jax.nn.silu(cond_ref[...]).astype(BF16)
    o_ref[...] = jnp.dot(a, w_ref[...].astype(BF16), preferred_element_type=F32) + b_ref[...]


def _modulation(cond, w_mod, b_mod):
    depth, d, n = w_mod.shape
    return pl.pallas_call(
        _mod_kernel,
        grid=(depth, n // MOD_TN),
        in_specs=[
            pl.BlockSpec((MOD_ROWS, d), lambda l, j: (0, 0)),
            pl.BlockSpec((None, d, MOD_TN), lambda l, j: (l, 0, j)),
            pl.BlockSpec((None, 1, MOD_TN), lambda l, j: (l, 0, j)),
        ],
        out_specs=pl.BlockSpec((None, MOD_ROWS, MOD_TN), lambda l, j: (l, 0, j)),
        out_shape=jax.ShapeDtypeStruct((depth, MOD_ROWS, n), F32),
        compiler_params=_params("arbitrary", "arbitrary"),
        name="modulation",
    )(cond, w_mod, b_mod.reshape(depth, 1, n))


def _rms_rope(xh, gain, cos, sin_lo, sin_hi):
    ms = jnp.mean(xh * xh, axis=-1, keepdims=True)
    y = xh * lax.rsqrt(ms + RMS_EPS) * gain
    return y * cos + pltpu.roll(y, HEAD_DIM - ROPE_FREQS, 1) * sin_lo + pltpu.roll(y, ROPE_FREQS, 1) * sin_hi


def _inproj_kernel(x_ref, m_ref, w_ref, alng_ref, alnb_ref, ws_ref, bs_ref, qg_ref, kg_ref, rope_ref,
                   oa_ref, z_ref, q_ref, k_ref, v_ref):
    tm = x_ref.shape[0]
    h = _ln(x_ref[...]) * (1.0 + m_ref[1:2, :]) + m_ref[0:1, :]
    hb = h.astype(BF16)

    pa = jnp.dot(hb, w_ref[:, AB_OFF:B_OFF], preferred_element_type=F32)
    u = jax.nn.gelu(pa[:, :A_WIDTH])
    vn = (_ln(jax.nn.gelu(pa[:, A_WIDTH:])) * alng_ref[...] + alnb_ref[...]).astype(BF16)
    for c in range(tm // CHUNK):
        rows = slice(c * CHUNK, (c + 1) * CHUNK)
        for hh in range(A_HEADS):
            cols = slice(hh * HEAD_DIM, (hh + 1) * HEAD_DIM)
            mixed = jnp.dot(ws_ref[hh], vn[rows, cols], preferred_element_type=F32) + bs_ref[:, cols]
            oa_ref[rows, cols] = (u[rows, cols] * mixed).astype(oa_ref.dtype)

    pb = jnp.dot(hb, w_ref[:, B_OFF:Q_OFF], preferred_element_type=F32)
    z_ref[...] = pb[:, :B_WIDTH] * jax.nn.sigmoid(pb[:, B_WIDTH:])

    cos, sin_lo, sin_hi = rope_ref[0], rope_ref[1], rope_ref[2]
    pq = jnp.dot(hb, w_ref[:, Q_OFF:K_OFF], preferred_element_type=F32)
    q_gain = qg_ref[...] * ATTN_SCALE
    for hh in range(N_HEADS):
        cols = slice(hh * HEAD_DIM, (hh + 1) * HEAD_DIM)
        q_ref[:, cols] = _rms_rope(pq[:, cols], q_gain, cos, sin_lo, sin_hi).astype(q_ref.dtype)
    pkv = jnp.dot(hb, w_ref[:, K_OFF:IN_COLS], preferred_element_type=F32)
    for hh in range(N_KV_HEADS):
        cols = slice(hh * HEAD_DIM, (hh + 1) * HEAD_DIM)
        k_ref[:, cols] = _rms_rope(pkv[:, cols], kg_ref[...], cos, sin_lo, sin_hi).astype(k_ref.dtype)
    v_ref[...] = pkv[:, KV_WIDTH:].astype(v_ref.dtype)


def _inproj(xa, m, w_in, aln_g, aln_b, ws, bs_full, q_gain, k_gain, rope, *, n_lat_rows, seq_len):
    rows, d = xa.shape
    tm = ROW_TILE
    n_lat_tiles = n_lat_rows // tm
    tiles_per_seq = seq_len // tm
    n_batch = n_lat_rows // seq_len

    def mod_idx(i):
        return (jnp.where(i < n_lat_tiles, i // tiles_per_seq, n_batch), 0, 0)

    def rope_idx(i):
        return (0, jnp.where(i < n_lat_tiles, i % tiles_per_seq, tiles_per_seq), 0)

    row = lambda i: (i, 0)
    const2 = lambda i: (0, 0)
    return pl.pallas_call(
        _inproj_kernel,
        grid=(rows // tm,),
        in_specs=[
            pl.BlockSpec((tm, d), row),
            pl.BlockSpec((None, N_MOD, d), mod_idx),
            pl.BlockSpec((d, IN_COLS), const2),
            pl.BlockSpec((1, A_WIDTH), const2),
            pl.BlockSpec((1, A_WIDTH), const2),
            pl.BlockSpec((A_HEADS, CHUNK, CHUNK), lambda i: (0, 0, 0)),
            pl.BlockSpec((CHUNK, A_WIDTH), const2),
            pl.BlockSpec((1, HEAD_DIM), const2),
            pl.BlockSpec((1, HEAD_DIM), const2),
            pl.BlockSpec((3, tm, HEAD_DIM), rope_idx),
        ],
        out_specs=[
            pl.BlockSpec((tm, A_WIDTH), row),
            pl.BlockSpec((tm, B_WIDTH), row),
            pl.BlockSpec((tm, C_WIDTH), row),
            pl.BlockSpec((tm, KV_WIDTH), row),
            pl.BlockSpec((tm, KV_WIDTH), row),
        ],
        out_shape=[
            jax.ShapeDtypeStruct((rows, A_WIDTH), BF16),
            jax.ShapeDtypeStruct((rows, B_WIDTH), F32),
            jax.ShapeDtypeStruct((rows, C_WIDTH), BF16),
            jax.ShapeDtypeStruct((rows, KV_WIDTH), BF16),
            jax.ShapeDtypeStruct((rows, KV_WIDTH), BF16),
        ],
        compiler_params=_params("arbitrary"),
        name="inproj",
    )(xa, m, w_in, aln_g, aln_b, ws, bs_full, q_gain, k_gain, rope)


def _conv_kernel(zp_ref, zc_ref, zn_ref, w_ref, b_ref, g_ref, beta_ref, o_ref, ext_ref, *,
                 n_lat_tiles, lat_tiles_per_seq, ctx_tiles_per_seq):
    t = zc_ref.shape[0]
    i = pl.program_id(0)
    pos = jnp.where(i < n_lat_tiles, i % lat_tiles_per_seq, (i - n_lat_tiles) % ctx_tiles_per_seq)
    last = jnp.where(i < n_lat_tiles, lat_tiles_per_seq - 1, ctx_tiles_per_seq - 1)
    ext_ref[0:CONV_HALO, :] = jnp.where(pos > 0, zp_ref[...], 0.0)
    ext_ref[CONV_HALO:CONV_HALO + t, :] = zc_ref[...]
    ext_ref[CONV_HALO + t:, :] = jnp.where(pos < last, zn_ref[...], 0.0)
    acc = jnp.broadcast_to(b_ref[...], (t, B_WIDTH))
    for j in range(CONV_WIDTH):
        start = CONV_HALO - CONV_HALF + j
        acc = acc + w_ref[j:j + 1, :] * ext_ref[start:start + t, :]
    y = _ln(acc) * g_ref[...] + beta_ref[...]
    o_ref[...] = jax.nn.silu(y).astype(o_ref.dtype)


def _conv_group(z, w_dw, b_dw, ln_g, ln_b, *, n_rows, n_lat_rows, seq_len, ctx_len):
    t = ROW_TILE
    halo_per_tile = t // CONV_HALO
    n_halo_blocks = z.shape[0] // CONV_HALO
    const2 = lambda i: (0, 0)
    kern = functools.partial(_conv_kernel, n_lat_tiles=n_lat_rows // t, lat_tiles_per_seq=seq_len // t,
                             ctx_tiles_per_seq=ctx_len // t)
    return pl.pallas_call(
        kern,
        grid=(n_rows // t,),
        in_specs=[
            pl.BlockSpec((CONV_HALO, B_WIDTH), lambda i: (jnp.maximum(i * halo_per_tile - 1, 0), 0)),
            pl.BlockSpec((t, B_WIDTH), lambda i: (i, 0)),
            pl.BlockSpec((CONV_HALO, B_WIDTH),
                         lambda i: (jnp.minimum((i + 1) * halo_per_tile, n_halo_blocks - 1), 0)),
            pl.BlockSpec((CONV_WIDTH, B_WIDTH), const2),
            pl.BlockSpec((1, B_WIDTH), const2),
            pl.BlockSpec((1, B_WIDTH), const2),
            pl.BlockSpec((1, B_WIDTH), const2),
        ],
        out_specs=pl.BlockSpec((t, B_WIDTH), lambda i: (i, 0)),
        out_shape=jax.ShapeDtypeStruct((z.shape[0], B_WIDTH), BF16),
        scratch_shapes=[pltpu.VMEM((t + 2 * CONV_HALO, B_WIDTH), F32)],
        compiler_params=_params("arbitrary"),
        name="conv_group",
    )(z, z, z, w_dw, b_dw, ln_g, ln_b)


def _attn_kernel(q_ref, kl_ref, vl_ref, kc_ref, vc_ref, o_ref, *, nq_lat, n_kv_tiles):
    tq = q_ref.shape[0]
    q = jnp.concatenate([q_ref[:, g * HEAD_DIM:(g + 1) * HEAD_DIM] for g in range(KV_GROUP)], axis=0)

    def step(carry, k, v):
        m, l, acc = carry
        s = lax.dot_general(q, k, (((1,), (1,)), ((), ())), preferred_element_type=F32)
        m_new = jnp.maximum(m, jnp.max(s, axis=-1, keepdims=True))
        alpha = jnp.exp(m - m_new)
        p = jnp.exp(s - m_new)
        l_new = alpha * l + jnp.sum(p, axis=-1, keepdims=True)
        acc_new = alpha * acc + jnp.dot(p.astype(BF16), v, preferred_element_type=F32)
        return m_new, l_new, acc_new

    rows = KV_GROUP * tq
    init = (jnp.full((rows, 1), -jnp.inf, F32), jnp.zeros((rows, 1), F32), jnp.zeros((rows, HEAD_DIM), F32))
    carry = step(init, kc_ref[...], vc_ref[...])

    def body(j, carry):
        off = pl.multiple_of(j * ATTN_TK, ATTN_TK)
        return step(carry, kl_ref[pl.ds(off, ATTN_TK), :], vl_ref[pl.ds(off, ATTN_TK), :])

    n_steps = jnp.where(pl.program_id(2) < nq_lat, n_kv_tiles, 0)
    m, l, acc = lax.fori_loop(0, n_steps, body, carry)
    out = acc / l
    for g in range(KV_GROUP):
        o_ref[:, g * HEAD_DIM:(g + 1) * HEAD_DIM] = out[g * tq:(g + 1) * tq, :].astype(o_ref.dtype)


def _attention(q, k, v, *, n_batch, seq_len, ctx_len, with_ctx_queries):
    tq = ATTN_TQ
    nq_lat = seq_len // tq
    nq_ctx = ctx_len // tq if with_ctx_queries else 0
    ctx_block0 = (n_batch * seq_len) // ctx_len

    def q_idx(b, kv, qi):
        lat = b * nq_lat + qi
        ctx = n_batch * nq_lat + b * (ctx_len // tq) + (qi - nq_lat)
        return (jnp.where(qi < nq_lat, lat, ctx), kv)

    lat_idx = lambda b, kv, qi: (b, kv)
    ctx_idx = lambda b, kv, qi: (ctx_block0 + b, kv)
    kern = functools.partial(_attn_kernel, nq_lat=nq_lat, n_kv_tiles=seq_len // ATTN_TK)
    return pl.pallas_call(
        kern,
        grid=(n_batch, N_KV_HEADS, nq_lat + nq_ctx),
        in_specs=[
            pl.BlockSpec((tq, GROUP_Q_WIDTH), q_idx),
            pl.BlockSpec((seq_len, HEAD_DIM), lat_idx),
            pl.BlockSpec((seq_len, HEAD_DIM), lat_idx),
            pl.BlockSpec((ctx_len, HEAD_DIM), ctx_idx),
            pl.BlockSpec((ctx_len, HEAD_DIM), ctx_idx),
        ],
        out_specs=pl.BlockSpec((tq, GROUP_Q_WIDTH), q_idx),
        out_shape=jax.ShapeDtypeStruct(q.shape, BF16),
        compiler_params=_params("arbitrary", "arbitrary", "arbitrary"),
        name="attention",
    )(q, k, v, k, v)


def _outproj_kernel(oa_ref, ob_ref, oc_ref, w_ref, x_ref, m_ref, g_ref, b_ref, o_ref, mix_ref, *, alpha):
    mix_ref[:, :A_WIDTH] = oa_ref[...]
    mix_ref[:, A_WIDTH:A_WIDTH + B_WIDTH] = ob_ref[...]
    mix_ref[:, A_WIDTH + B_WIDTH:] = oc_ref[...]
    y = jnp.dot(mix_ref[...], w_ref[...], preferred_element_type=F32)
    r = alpha * x_ref[...] + m_ref[2:3, :] * y
    o_ref[...] = _ln(r) * g_ref[...] + b_ref[...]


def _row_mod_idx(tm, n_lat_rows, seq_len):
    n_lat_tiles = n_lat_rows // tm
    tiles_per_seq = seq_len // tm
    n_batch = n_lat_rows // seq_len
    return lambda i, *_: (jnp.where(i < n_lat_tiles, i // tiles_per_seq, n_batch), 0, 0)


def _outproj(oa, ob, oc, w_out, xa, m, ln_g, ln_b, *, n_rows, n_lat_rows, seq_len, alpha):
    d = xa.shape[1]
    tm = OUT_TM
    row = lambda i: (i, 0)
    const2 = lambda i: (0, 0)
    return pl.pallas_call(
        functools.partial(_outproj_kernel, alpha=alpha),
        grid=(n_rows // tm,),
        in_specs=[
            pl.BlockSpec((tm, A_WIDTH), row),
            pl.BlockSpec((tm, B_WIDTH), row),
            pl.BlockSpec((tm, C_WIDTH), row),
            pl.BlockSpec(w_out.shape, const2),
            pl.BlockSpec((tm, d), row),
            pl.BlockSpec((None, N_MOD, d), _row_mod_idx(tm, n_lat_rows, seq_len)),
            pl.BlockSpec((1, d), const2),
            pl.BlockSpec((1, d), const2),
        ],
        out_specs=pl.BlockSpec((tm, d), row),
        out_shape=jax.ShapeDtypeStruct((n_rows, d), F32),
        scratch_shapes=[pltpu.VMEM((tm, w_out.shape[0]), BF16)],
        compiler_params=_params("arbitrary"),
        name="outproj",
    )(oa, ob, oc, w_out, xa, m, ln_g, ln_b)


def _ff_kernel(x_ref, m_ref, w1_ref, w2_ref, g_ref, b_ref, o_ref, xm_ref, acc_ref, *, alpha):
    j = pl.program_id(1)

    @pl.when(j == 0)
    def _():
        xm_ref[...] = (_ln(x_ref[...]) * (1.0 + m_ref[4:5, :]) + m_ref[3:4, :]).astype(BF16)

    h = jnp.dot(xm_ref[...], w1_ref[...], preferred_element_type=F32)
    a = jnp.square(jnp.maximum(h, 0.0)).astype(BF16)
    part = jnp.dot(a, w2_ref[...], preferred_element_type=F32)

    @pl.when(j == 0)
    def _():
        acc_ref[...] = part

    @pl.when(j > 0)
    def _():
        acc_ref[...] += part

    @pl.when(j == pl.num_programs(1) - 1)
    def _():
        r = alpha * x_ref[...] + m_ref[5:6, :] * acc_ref[...]
        o_ref[...] = _ln(r) * g_ref[...] + b_ref[...]


def _ff(xa, m, w1, w2, ln_g, ln_b, *, n_rows, n_lat_rows, seq_len, alpha):
    d = xa.shape[1]
    d_ff = w1.shape[1]
    tm, tf = FF_TM, FF_TF
    row = lambda i, j: (i, 0)
    const2 = lambda i, j: (0, 0)
    return pl.pallas_call(
        functools.partial(_ff_kernel, alpha=alpha),
        grid=(n_rows // tm, d_ff // tf),
        in_specs=[
            pl.BlockSpec((tm, d), row),
            pl.BlockSpec((None, N_MOD, d), _row_mod_idx(tm, n_lat_rows, seq_len)),
            pl.BlockSpec((d, tf), lambda i, j: (0, j)),
            pl.BlockSpec((tf, d), lambda i, j: (j, 0)),
            pl.BlockSpec((1, d), const2),
            pl.BlockSpec((1, d), const2),
        ],
        out_specs=pl.BlockSpec((tm, d), row),
        out_shape=jax.ShapeDtypeStruct((n_rows, d), F32),
        scratch_shapes=[pltpu.VMEM((tm, d), BF16), pltpu.VMEM((tm, d), F32)],
        compiler_params=_params("arbitrary", "arbitrary"),
        name="ff",
    )(xa, m, w1, w2, ln_g, ln_b)


def _rope_table(seq_len, tile):
    t = jnp.arange(seq_len, dtype=jnp.int32)
    row = (t // GRID_W).astype(F32)
    col = (t % GRID_W).astype(F32)
    freqs = ROPE_THETA ** (-jnp.arange(ROPE_FREQS, dtype=F32) / ROPE_FREQS)
    ar, ac = row[:, None] * freqs, col[:, None] * freqs
    zeros = jnp.zeros_like(ar)
    cos = jnp.concatenate([jnp.cos(ar), jnp.cos(ar), jnp.cos(ac), jnp.cos(ac)], axis=1)
    sin_lo = jnp.concatenate([-jnp.sin(ar), zeros, -jnp.sin(ac), zeros], axis=1)
    sin_hi = jnp.concatenate([zeros, jnp.sin(ar), zeros, jnp.sin(ac)], axis=1)
    table = jnp.stack([cos, sin_lo, sin_hi])
    ident = jnp.stack([jnp.ones((tile, HEAD_DIM), F32), jnp.zeros((tile, HEAD_DIM), F32),
                       jnp.zeros((tile, HEAD_DIM), F32)])
    return jnp.concatenate([table, ident], axis=1)


def kernel(x, c, ctx, c_ctx, w_mod, b_mod, w_in, a_ln_g, a_ln_b, a_ws, a_bs, b_dw, b_dw_bias, b_ln_g, b_ln_b,
           q_gain, k_gain, w_out, ln1_g, ln1_b, w_ff1, w_ff2, ln2_g, ln2_b):
    n_batch, seq_len, d = x.shape
    ctx_len = ctx.shape[1]
    depth = w_mod.shape[0]
    assert w_in.shape[2] == IN_COLS and w_out.shape[1] == A_WIDTH + B_WIDTH + C_WIDTH
    assert seq_len % ATTN_TK == 0 and seq_len % FF_TM == 0 and ctx_len % ROW_TILE == 0
    assert (n_batch * ctx_len) % FF_TM == 0 and n_batch + 1 <= MOD_ROWS
    alpha = float((2 * depth) ** 0.25)
    n_lat = n_batch * seq_len
    n_all = n_lat + n_batch * ctx_len

    cond = jnp.zeros((MOD_ROWS, d), F32).at[:n_batch].set(c).at[n_batch].set(c_ctx)
    m_all = _modulation(cond, w_mod, b_mod).reshape(depth, MOD_ROWS, N_MOD, d)
    rope = _rope_table(seq_len, ROW_TILE)
    xa = jnp.concatenate([x.reshape(n_lat, d), ctx.reshape(n_batch * ctx_len, d)], axis=0)
    geom = dict(n_lat_rows=n_lat, seq_len=seq_len)

    for l in range(depth):
        last = l == depth - 1
        n_rows = n_lat if last else n_all
        m = m_all[l]
        bs_full = jnp.repeat(a_bs[l].T, HEAD_DIM, axis=1)
        oa, z, q, k, v = _inproj(
            xa, m, w_in[l].astype(BF16), a_ln_g[l][None], a_ln_b[l][None], a_ws[l].astype(BF16), bs_full,
            q_gain[l][None], k_gain[l][None], rope, **geom)
        ob = _conv_group(z, b_dw[l], b_dw_bias[l][None], b_ln_g[l][None], b_ln_b[l][None],
                         n_rows=n_rows, ctx_len=ctx_len, **geom)
        oc = _attention(q, k, v, n_batch=n_batch, seq_len=seq_len, ctx_len=ctx_len, with_ctx_queries=not last)
        x1 = _outproj(oa, ob, oc, w_out[l].astype(BF16), xa, m, ln1_g[l][None], ln1_b[l][None],
                      n_rows=n_rows, alpha=alpha, **geom)
        xa = _ff(x1, m, w_ff1[l].astype(BF16), w_ff2[l].astype(BF16), ln2_g[l][None], ln2_b[l][None],
                 n_rows=n_rows, alpha=alpha, **geom)
    return xa.reshape(n_batch, seq_len, d)
```

```python
import functools
import math

import jax
import jax.numpy as jnp
from jax import lax
from jax.experimental import pallas as pl
from jax.experimental.pallas import tpu as pltpu

HEAD_DIM = 128
CHUNK = 128
A_HEADS = 4
A_WIDTH = A_HEADS * HEAD_DIM
B_WIDTH = 512
CONV_WIDTH = 31
CONV_HALF = CONV_WIDTH // 2
N_HEADS = 8
N_KV_HEADS = 2
KV_GROUP = N_HEADS // N_KV_HEADS
C_WIDTH = N_HEADS * HEAD_DIM
KV_WIDTH = N_KV_HEADS * HEAD_DIM
GROUP_Q_WIDTH = KV_GROUP * HEAD_DIM
VX_WIDTH = 2 * HEAD_DIM
GRID_W = 64
ROPE_FREQS = 32
ROPE_THETA = 10000.0
ATTN_SCALE = HEAD_DIM ** -0.5
LOG2E = math.log2(math.e)
N_MOD = 6
LN_EPS = 1e-6
RMS_EPS = 1e-6
AB_OFF = 0
B_OFF = 2 * A_WIDTH
Q_OFF = B_OFF + 2 * B_WIDTH
K_OFF = Q_OFF + C_WIDTH
IN_COLS = K_OFF + 2 * KV_WIDTH
UNSHIFTED_SOFTMAX_MAX_LOG2 = 60.0

MOD_ROWS = 8
MOD_TN = 512
INPROJ_TM = 256
ROW_TILE = 256
CONV_HALO = 16
ATTN_TQ = 256
ATTN_TK = 1024
OUT_TM = 512
FF_TM = 512
FF_TF = 512
VMEM_LIMIT = 56 * 1024 * 1024

BF16 = jnp.bfloat16
F32 = jnp.float32


def _params(*sem):
    return pltpu.CompilerParams(dimension_semantics=sem, vmem_limit_bytes=VMEM_LIMIT)


def _ln(x):
    mu = jnp.mean(x, axis=-1, keepdims=True)
    xc = x - mu
    var = jnp.mean(xc * xc, axis=-1, keepdims=True)
    return xc * lax.rsqrt(var + LN_EPS)


def _split_row_specs(tm, d, n_lat_tiles, ctx_tile0):
    lat = pl.BlockSpec((tm, d), lambda i, *_: (jnp.minimum(i, n_lat_tiles - 1), 0))
    ctx = pl.BlockSpec((tm, d), lambda i, *_: (ctx_tile0 + jnp.maximum(i - n_lat_tiles, 0), 0))
    return lat, ctx


def _row_mod_idx(tm, n_lat_rows, seq_len):
    n_lat_tiles = n_lat_rows // tm
    tiles_per_seq = seq_len // tm
    n_batch = n_lat_rows // seq_len
    return lambda i, *_: (jnp.where(i < n_lat_tiles, i // tiles_per_seq, n_batch), 0, 0)


def _mod_kernel(cond_ref, w_ref, b_ref, o_ref):
    a = jax.nn.silu(cond_ref[...]).astype(BF16)
    o_ref[...] = jnp.dot(a, w_ref[...].astype(BF16), preferred_element_type=F32) + b_ref[...]


def _modulation(cond, w_mod, b_mod):
    depth, d, n = w_mod.shape
    return pl.pallas_call(
        _mod_kernel,
        grid=(depth, n // MOD_TN),
        in_specs=[
            pl.BlockSpec((MOD_ROWS, d), lambda l, j: (0, 0)),
            pl.BlockSpec((None, d, MOD_TN), lambda l, j: (l, 0, j)),
            pl.BlockSpec((None, 1, MOD_TN), lambda l, j: (l, 0, j)),
        ],
        out_specs=pl.BlockSpec((None, MOD_ROWS, MOD_TN), lambda l, j: (l, 0, j)),
        out_shape=jax.ShapeDtypeStruct((depth, MOD_ROWS, n), F32),
        compiler_params=_params("arbitrary", "arbitrary"),
        name="modulation",
    )(cond, w_mod, b_mod.reshape(depth, 1, n))


def _rms_rope(xh, gain, cos, sin_lo, sin_hi):
    ms = jnp.mean(xh * xh, axis=-1, keepdims=True)
    y = xh * lax.rsqrt(ms + RMS_EPS) * gain
    return y * cos + pltpu.roll(y, HEAD_DIM - ROPE_FREQS, 1) * sin_lo + pltpu.roll(y, ROPE_FREQS, 1) * sin_hi


def _inproj_kernel(xl_ref, xc_ref, m_ref, w_ref, alng_ref, alnb_ref, ws_ref, bs_ref, qg_ref, kg_ref, rope_ref,
                   oa_ref, z_ref, q_ref, k_ref, vx_ref, *, n_lat_tiles):
    tm = xl_ref.shape[0]
    x = jnp.where(pl.program_id(0) < n_lat_tiles, xl_ref[...], xc_ref[...])
    h = _ln(x) * (1.0 + m_ref[1:2, :]) + m_ref[0:1, :]
    hb = h.astype(BF16)

    pa = jnp.dot(hb, w_ref[:, AB_OFF:B_OFF], preferred_element_type=F32)
    u = jax.nn.gelu(pa[:, :A_WIDTH])
    vn = (_ln(jax.nn.gelu(pa[:, A_WIDTH:])) * alng_ref[...] + alnb_ref[...]).astype(BF16)
    for c in range(tm // CHUNK):
        rows = slice(c * CHUNK, (c + 1) * CHUNK)
        for hh in range(A_HEADS):
            cols = slice(hh * HEAD_DIM, (hh + 1) * HEAD_DIM)
            mixed = jnp.dot(ws_ref[hh], vn[rows, cols], preferred_element_type=F32) + bs_ref[:, cols]
            oa_ref[rows, cols] = (u[rows, cols] * mixed).astype(oa_ref.dtype)

    pb = jnp.dot(hb, w_ref[:, B_OFF:Q_OFF], preferred_element_type=F32)
    z_ref[...] = pb[:, :B_WIDTH] * jax.nn.sigmoid(pb[:, B_WIDTH:])

    cos, sin_lo, sin_hi = rope_ref[0], rope_ref[1], rope_ref[2]
    pq = jnp.dot(hb, w_ref[:, Q_OFF:K_OFF], preferred_element_type=F32)
    q_gain = qg_ref[...] * (ATTN_SCALE * LOG2E)
    for hh in range(N_HEADS):
        cols = slice(hh * HEAD_DIM, (hh + 1) * HEAD_DIM)
        q_ref[:, cols] = _rms_rope(pq[:, cols], q_gain, cos, sin_lo, sin_hi).astype(q_ref.dtype)
    pkv = jnp.dot(hb, w_ref[:, K_OFF:IN_COLS], preferred_element_type=F32)
    one_hot = (lax.broadcasted_iota(jnp.int32, (tm, HEAD_DIM), 1) == 0).astype(vx_ref.dtype)
    for hh in range(N_KV_HEADS):
        cols = slice(hh * HEAD_DIM, (hh + 1) * HEAD_DIM)
        k_ref[:, cols] = _rms_rope(pkv[:, cols], kg_ref[...], cos, sin_lo, sin_hi).astype(k_ref.dtype)
        vx_ref[:, hh * VX_WIDTH:hh * VX_WIDTH + HEAD_DIM] = pkv[:, KV_WIDTH + hh * HEAD_DIM:
                                                                KV_WIDTH + (hh + 1) * HEAD_DIM].astype(vx_ref.dtype)
        vx_ref[:, hh * VX_WIDTH + HEAD_DIM:(hh + 1) * VX_WIDTH] = one_hot


def _inproj(x_lat, x_ctx, ctx_tile0, m, w_in, layer, aln_g, aln_b, ws, bs_full, q_gain, k_gain, rope, *,
            n_rows, n_lat_rows, seq_len):
    d = x_lat.shape[1]
    tm = INPROJ_TM
    n_lat_tiles = n_lat_rows // tm
    tiles_per_seq = seq_len // tm
    n_batch = n_lat_rows // seq_len

    def mod_idx(i):
        return (jnp.where(i < n_lat_tiles, i // tiles_per_seq, n_batch), 0, 0)

    def rope_idx(i):
        return (0, jnp.where(i < n_lat_tiles, i % tiles_per_seq, tiles_per_seq), 0)

    row = lambda i: (i, 0)
    const2 = lambda i: (0, 0)
    lat_spec, ctx_spec = _split_row_specs(tm, d, n_lat_tiles, ctx_tile0)
    return pl.pallas_call(
        functools.partial(_inproj_kernel, n_lat_tiles=n_lat_tiles),
        grid=(n_rows // tm,),
        in_specs=[
            lat_spec,
            ctx_spec,
            pl.BlockSpec((None, N_MOD, d), mod_idx),
            pl.BlockSpec((None, d, IN_COLS), lambda i: (layer, 0, 0), pipeline_mode=pl.Buffered(1)),
            pl.BlockSpec((1, A_WIDTH), const2),
            pl.BlockSpec((1, A_WIDTH), const2),
            pl.BlockSpec((None, A_HEADS, CHUNK, CHUNK), lambda i: (layer, 0, 0, 0)),
            pl.BlockSpec((CHUNK, A_WIDTH), const2),
            pl.BlockSpec((1, HEAD_DIM), const2),
            pl.BlockSpec((1, HEAD_DIM), const2),
            pl.BlockSpec((3, tm, HEAD_DIM), rope_idx),
        ],
        out_specs=[
            pl.BlockSpec((tm, A_WIDTH), row),
            pl.BlockSpec((tm, B_WIDTH), row),
            pl.BlockSpec((tm, C_WIDTH), row),
            pl.BlockSpec((tm, KV_WIDTH), row),
            pl.BlockSpec((tm, N_KV_HEADS * VX_WIDTH), row),
        ],
        out_shape=[
            jax.ShapeDtypeStruct((n_rows, A_WIDTH), BF16),
            jax.ShapeDtypeStruct((n_rows, B_WIDTH), F32),
            jax.ShapeDtypeStruct((n_rows, C_WIDTH), BF16),
            jax.ShapeDtypeStruct((n_rows, KV_WIDTH), BF16),
            jax.ShapeDtypeStruct((n_rows, N_KV_HEADS * VX_WIDTH), BF16),
        ],
        compiler_params=_params("arbitrary"),
        name="inproj",
    )(x_lat, x_ctx, m, w_in, aln_g, aln_b, ws, bs_full, q_gain, k_gain, rope)


def _conv_kernel(zp_ref, zc_ref, zn_ref, w_ref, b_ref, g_ref, beta_ref, o_ref, ext_ref, *,
                 n_lat_tiles, lat_tiles_per_seq, ctx_tiles_per_seq):
    t = zc_ref.shape[0]
    i = pl.program_id(0)
    pos = jnp.where(i < n_lat_tiles, i % lat_tiles_per_seq, (i - n_lat_tiles) % ctx_tiles_per_seq)
    last = jnp.where(i < n_lat_tiles, lat_tiles_per_seq - 1, ctx_tiles_per_seq - 1)
    ext_ref[0:CONV_HALO, :] = jnp.where(pos > 0, zp_ref[...], 0.0)
    ext_ref[CONV_HALO:CONV_HALO + t, :] = zc_ref[...]
    ext_ref[CONV_HALO + t:, :] = jnp.where(pos < last, zn_ref[...], 0.0)
    acc = jnp.broadcast_to(b_ref[...], (t, B_WIDTH))
    first = CONV_HALO - CONV_HALF
    n_ext = ext_ref.shape[0]
    ext = ext_ref[...]
    for r in range(8):
        taps = [j for j in range(CONV_WIDTH) if (first + j) % 8 == r]
        if not taps:
            continue
        shifted = ext if r == 0 else pltpu.roll(ext, n_ext - r, 0)
        for j in taps:
            q = (first + j) // 8 * 8
            acc = acc + w_ref[j:j + 1, :] * shifted[q:q + t, :]
    y = _ln(acc) * g_ref[...] + beta_ref[...]
    o_ref[...] = jax.nn.silu(y).astype(o_ref.dtype)


def _conv_group(z, w_dw, b_dw, ln_g, ln_b, *, n_rows, n_lat_rows, seq_len, ctx_len):
    t = ROW_TILE
    halo_per_tile = t // CONV_HALO
    n_halo_blocks = z.shape[0] // CONV_HALO
    const2 = lambda i: (0, 0)
    kern = functools.partial(_conv_kernel, n_lat_tiles=n_lat_rows // t, lat_tiles_per_seq=seq_len // t,
                             ctx_tiles_per_seq=ctx_len // t)
    return pl.pallas_call(
        kern,
        grid=(n_rows // t,),
        in_specs=[
            pl.BlockSpec((CONV_HALO, B_WIDTH), lambda i: (jnp.maximum(i * halo_per_tile - 1, 0), 0)),
            pl.BlockSpec((t, B_WIDTH), lambda i: (i, 0)),
            pl.BlockSpec((CONV_HALO, B_WIDTH),
                         lambda i: (jnp.minimum((i + 1) * halo_per_tile, n_halo_blocks - 1), 0)),
            pl.BlockSpec((CONV_WIDTH, B_WIDTH), const2),
            pl.BlockSpec((1, B_WIDTH), const2),
            pl.BlockSpec((1, B_WIDTH), const2),
            pl.BlockSpec((1, B_WIDTH), const2),
        ],
        out_specs=pl.BlockSpec((t, B_WIDTH), lambda i: (i, 0)),
        out_shape=jax.ShapeDtypeStruct((z.shape[0], B_WIDTH), BF16),
        scratch_shapes=[pltpu.VMEM((t + 2 * CONV_HALO, B_WIDTH), F32)],
        compiler_params=_params("arbitrary"),
        name="conv_group",
    )(z, z, z, w_dw, b_dw, ln_g, ln_b)


def _attn_kernel(unshifted_ref, q_ref, kl_ref, vl_ref, kc_ref, vc_ref, o_ref, *, nq_lat, n_kv_tiles):
    tq = q_ref.shape[0]
    rows = KV_GROUP * tq
    q = jnp.concatenate([q_ref[:, g * HEAD_DIM:(g + 1) * HEAD_DIM] for g in range(KV_GROUP)], axis=0)
    n_steps = jnp.where(pl.program_id(2) < nq_lat, n_kv_tiles, 0)

    def scores(k):
        return lax.dot_general(q, k, (((1,), (1,)), ((), ())), preferred_element_type=F32)

    def latent_tile(j):
        off = pl.multiple_of(j * ATTN_TK, ATTN_TK)
        return kl_ref[pl.ds(off, ATTN_TK), :], vl_ref[pl.ds(off, ATTN_TK), :]

    def write(out):
        for g in range(KV_GROUP):
            o_ref[:, g * HEAD_DIM:(g + 1) * HEAD_DIM] = out[g * tq:(g + 1) * tq, :].astype(o_ref.dtype)

    def unshifted_tile(k, vx):
        return jnp.dot(jnp.exp2(scores(k)).astype(BF16), vx, preferred_element_type=F32)

    def write_normalised(acc):
        write(acc[:, :HEAD_DIM] / acc[:, HEAD_DIM:HEAD_DIM + 1])

    unshifted = unshifted_ref[0] != 0
    is_latent = pl.program_id(2) < nq_lat

    @pl.when(jnp.logical_and(unshifted, is_latent))
    def _():
        acc = unshifted_tile(kc_ref[...], vc_ref[...])
        for j in range(n_kv_tiles):
            tile = slice(j * ATTN_TK, (j + 1) * ATTN_TK)
            acc = acc + unshifted_tile(kl_ref[tile, :], vl_ref[tile, :])
        write_normalised(acc)

    @pl.when(jnp.logical_and(unshifted, jnp.logical_not(is_latent)))
    def _():
        write_normalised(unshifted_tile(kc_ref[...], vc_ref[...]))

    @pl.when(unshifted_ref[0] == 0)
    def _():
        def step(carry, k, vx):
            m, l, acc = carry
            s = scores(k)
            m_new = jnp.maximum(m, jnp.max(s, axis=-1, keepdims=True))
            alpha = jnp.exp2(m - m_new)
            p = jnp.exp2(s - m_new)
            l_new = alpha * l + jnp.sum(p, axis=-1, keepdims=True)
            acc_new = alpha * acc + jnp.dot(p.astype(BF16), vx[:, :HEAD_DIM], preferred_element_type=F32)
            return m_new, l_new, acc_new

        init = (jnp.full((rows, 1), -jnp.inf, F32), jnp.zeros((rows, 1), F32), jnp.zeros((rows, HEAD_DIM), F32))
        carry = step(init, kc_ref[...], vc_ref[...])
        m, l, acc = lax.fori_loop(0, n_steps, lambda j, cr: step(cr, *latent_tile(j)), carry)
        write(acc / l)


def _attention(unshifted, q, k, vx, *, n_batch, seq_len, ctx_len, with_ctx_queries):
    tq = ATTN_TQ
    nq_lat = seq_len // tq
    nq_ctx = ctx_len // tq if with_ctx_queries else 0
    ctx_block0 = (n_batch * seq_len) // ctx_len

    def q_idx(b, kv, qi):
        lat = b * nq_lat + qi
        ctx = n_batch * nq_lat + b * (ctx_len // tq) + (qi - nq_lat)
        return (jnp.where(qi < nq_lat, lat, ctx), kv)

    lat_idx = lambda b, kv, qi: (b, kv)
    ctx_idx = lambda b, kv, qi: (ctx_block0 + b, kv)
    kern = functools.partial(_attn_kernel, nq_lat=nq_lat, n_kv_tiles=seq_len // ATTN_TK)
    return pl.pallas_call(
        kern,
        grid=(n_batch, N_KV_HEADS, nq_lat + nq_ctx),
        in_specs=[
            pl.BlockSpec(memory_space=pltpu.SMEM),
            pl.BlockSpec((tq, GROUP_Q_WIDTH), q_idx),
            pl.BlockSpec((seq_len, HEAD_DIM), lat_idx),
            pl.BlockSpec((seq_len, VX_WIDTH), lat_idx),
            pl.BlockSpec((ctx_len, HEAD_DIM), ctx_idx),
            pl.BlockSpec((ctx_len, VX_WIDTH), ctx_idx),
        ],
        out_specs=pl.BlockSpec((tq, GROUP_Q_WIDTH), q_idx),
        out_shape=jax.ShapeDtypeStruct(q.shape, BF16),
        compiler_params=_params("arbitrary", "arbitrary", "arbitrary"),
        name="attention",
    )(unshifted, q, k, vx, k, vx)


def _outproj_kernel(oa_ref, ob_ref, oc_ref, w_ref, xl_ref, xc_ref, m_ref, g_ref, b_ref, o_ref, mix_ref, *,
                    alpha, n_lat_tiles):
    mix_ref[:, :A_WIDTH] = oa_ref[...]
    mix_ref[:, A_WIDTH:A_WIDTH + B_WIDTH] = ob_ref[...]
    mix_ref[:, A_WIDTH + B_WIDTH:] = oc_ref[...]
    y = jnp.dot(mix_ref[...], w_ref[...], preferred_element_type=F32)
    x = jnp.where(pl.program_id(0) < n_lat_tiles, xl_ref[...], xc_ref[...])
    r = alpha * x + m_ref[2:3, :] * y
    o_ref[...] = _ln(r) * g_ref[...] + b_ref[...]


def _outproj(oa, ob, oc, w_out, layer, x_lat, x_ctx, ctx_tile0, m, ln_g, ln_b, *, n_rows, n_lat_rows, seq_len, alpha):
    d = x_lat.shape[1]
    tm = OUT_TM
    n_lat_tiles = n_lat_rows // tm
    row = lambda i: (i, 0)
    const2 = lambda i: (0, 0)
    lat_spec, ctx_spec = _split_row_specs(tm, d, n_lat_tiles, ctx_tile0)
    return pl.pallas_call(
        functools.partial(_outproj_kernel, alpha=alpha, n_lat_tiles=n_lat_tiles),
        grid=(n_rows // tm,),
        in_specs=[
            pl.BlockSpec((tm, A_WIDTH), row),
            pl.BlockSpec((tm, B_WIDTH), row),
            pl.BlockSpec((tm, C_WIDTH), row),
            pl.BlockSpec((None,) + w_out.shape[1:], lambda i: (layer, 0, 0), pipeline_mode=pl.Buffered(1)),
            lat_spec,
            ctx_spec,
            pl.BlockSpec((None, N_MOD, d), _row_mod_idx(tm, n_lat_rows, seq_len)),
            pl.BlockSpec((1, d), const2),
            pl.BlockSpec((1, d), const2),
        ],
        out_specs=pl.BlockSpec((tm, d), row),
        out_shape=jax.ShapeDtypeStruct((n_rows, d), F32),
        scratch_shapes=[pltpu.VMEM((tm, w_out.shape[1]), BF16)],
        compiler_params=_params("arbitrary"),
        name="outproj",
    )(oa, ob, oc, w_out, x_lat, x_ctx, m, ln_g, ln_b)


def _ff_kernel(x_ref, m_ref, w1_ref, w2_ref, g_ref, b_ref, o_ref, xm_ref, acc_ref, *, alpha):
    j = pl.program_id(1)

    @pl.when(j == 0)
    def _():
        xm_ref[...] = (_ln(x_ref[...]) * (1.0 + m_ref[4:5, :]) + m_ref[3:4, :]).astype(BF16)
        acc_ref[...] = jnp.zeros_like(acc_ref)

    h = jnp.dot(xm_ref[...], w1_ref[...], preferred_element_type=F32)
    a = jnp.square(jnp.maximum(h, 0.0)).astype(BF16)
    acc_ref[...] += jnp.dot(a, w2_ref[...], preferred_element_type=F32)

    @pl.when(j == pl.num_programs(1) - 1)
    def _():
        r = alpha * x_ref[...] + m_ref[5:6, :] * acc_ref[...]
        o_ref[...] = _ln(r) * g_ref[...] + b_ref[...]


def _ff(xa, m, w1, w2, layer, ln_g, ln_b, *, n_rows, n_lat_rows, seq_len, alpha):
    d = xa.shape[1]
    d_ff = w1.shape[2]
    tm, tf = FF_TM, FF_TF
    row = lambda i, j: (i, 0)
    const2 = lambda i, j: (0, 0)
    return pl.pallas_call(
        functools.partial(_ff_kernel, alpha=alpha),
        grid=(n_rows // tm, d_ff // tf),
        in_specs=[
            pl.BlockSpec((tm, d), row),
            pl.BlockSpec((None, N_MOD, d), _row_mod_idx(tm, n_lat_rows, seq_len)),
            pl.BlockSpec((None, d, tf), lambda i, j: (layer, 0, j)),
            pl.BlockSpec((None, tf, d), lambda i, j: (layer, j, 0)),
            pl.BlockSpec((1, d), const2),
            pl.BlockSpec((1, d), const2),
        ],
        out_specs=pl.BlockSpec((tm, d), row),
        out_shape=jax.ShapeDtypeStruct((n_rows, d), F32),
        scratch_shapes=[pltpu.VMEM((tm, d), BF16), pltpu.VMEM((tm, d), F32)],
        compiler_params=_params("arbitrary", "arbitrary"),
        name="ff",
    )(xa, m, w1, w2, ln_g, ln_b)


def _rope_table(seq_len, tile):
    t = jnp.arange(seq_len, dtype=jnp.int32)
    row = (t // GRID_W).astype(F32)
    col = (t % GRID_W).astype(F32)
    freqs = ROPE_THETA ** (-jnp.arange(ROPE_FREQS, dtype=F32) / ROPE_FREQS)
    ar, ac = row[:, None] * freqs, col[:, None] * freqs
    zeros = jnp.zeros_like(ar)
    cos = jnp.concatenate([jnp.cos(ar), jnp.cos(ar), jnp.cos(ac), jnp.cos(ac)], axis=1)
    sin_lo = jnp.concatenate([-jnp.sin(ar), zeros, -jnp.sin(ac), zeros], axis=1)
    sin_hi = jnp.concatenate([zeros, jnp.sin(ar), zeros, jnp.sin(ac)], axis=1)
    table = jnp.stack([cos, sin_lo, sin_hi])
    ident = jnp.stack([jnp.ones((tile, HEAD_DIM), F32), jnp.zeros((tile, HEAD_DIM), F32),
                       jnp.zeros((tile, HEAD_DIM), F32)])
    return jnp.concatenate([table, ident], axis=1)


def kernel(x, c, ctx, c_ctx, w_mod, b_mod, w_in, a_ln_g, a_ln_b, a_ws, a_bs, b_dw, b_dw_bias, b_ln_g, b_ln_b,
           q_gain, k_gain, w_out, ln1_g, ln1_b, w_ff1, w_ff2, ln2_g, ln2_b):
    n_batch, seq_len, d = x.shape
    ctx_len = ctx.shape[1]
    depth = w_mod.shape[0]
    assert w_in.shape[2] == IN_COLS and w_out.shape[1] == A_WIDTH + B_WIDTH + C_WIDTH
    assert seq_len % ATTN_TK == 0 and seq_len % FF_TM == 0 and ctx_len % ROW_TILE == 0 and ctx_len % ATTN_TQ == 0
    assert seq_len % INPROJ_TM == 0 and (n_batch * ctx_len) % INPROJ_TM == 0
    assert (n_batch * ctx_len) % FF_TM == 0 and n_batch + 1 <= MOD_ROWS
    alpha = float((2 * depth) ** 0.25)
    n_lat = n_batch * seq_len
    n_ctx = n_batch * ctx_len
    n_all = n_lat + n_ctx

    cond = jnp.zeros((MOD_ROWS, d), F32).at[:n_batch].set(c).at[n_batch].set(c_ctx)
    m_all = _modulation(cond, w_mod, b_mod).reshape(depth, MOD_ROWS, N_MOD, d)
    rope = _rope_table(seq_len, INPROJ_TM)
    w_in_b, w_out_b, ws_b = w_in.astype(BF16), w_out.astype(BF16), a_ws.astype(BF16)
    w1_b, w2_b = w_ff1.astype(BF16), w_ff2.astype(BF16)
    score_bound = HEAD_DIM * ATTN_SCALE * LOG2E * jnp.max(jnp.abs(q_gain), axis=1) * jnp.max(jnp.abs(k_gain), axis=1)
    unshifted = (score_bound <= UNSHIFTED_SOFTMAX_MAX_LOG2).astype(jnp.int32)
    geom = dict(n_lat_rows=n_lat, seq_len=seq_len)

    x_lat, x_ctx, ctx_row0 = x.reshape(n_lat, d), ctx.reshape(n_ctx, d), 0
    for l in range(depth):
        last = l == depth - 1
        n_rows = n_lat if last else n_all
        m = m_all[l]
        bs_full = jnp.repeat(a_bs[l].T, HEAD_DIM, axis=1)
        oa, z, q, k, vx = _inproj(
            x_lat, x_ctx, ctx_row0 // INPROJ_TM, m, w_in_b, l, a_ln_g[l][None], a_ln_b[l][None], ws_b, bs_full,
            q_gain[l][None], k_gain[l][None], rope, n_rows=n_all, **geom)
        ob = _conv_group(z, b_dw[l], b_dw_bias[l][None], b_ln_g[l][None], b_ln_b[l][None],
                         n_rows=n_rows, ctx_len=ctx_len, **geom)
        oc = _attention(unshifted[l:l + 1], q, k, vx, n_batch=n_batch, seq_len=seq_len, ctx_len=ctx_len,
                        with_ctx_queries=not last)
        x1 = _outproj(oa, ob, oc, w_out_b, l, x_lat, x_ctx, ctx_row0 // OUT_TM, m, ln1_g[l][None], ln1_b[l][None],
                      n_rows=n_rows, alpha=alpha, **geom)
        xa = _ff(x1, m, w1_b, w2_b, l, ln2_g[l][None], ln2_b[l][None], n_rows=n_rows, alpha=alpha, **geom)
        x_lat, x_ctx, ctx_row0 = xa, xa, n_lat
    return xa.reshape(n_batch, seq_len, d)
```

```python
import functools
import math

import jax
import jax.numpy as jnp
from jax import lax
from jax.experimental import pallas as pl
from jax.experimental.pallas import tpu as pltpu

HEAD_DIM = 128
CHUNK = 128
A_HEADS = 4
A_WIDTH = A_HEADS * HEAD_DIM
B_WIDTH = 512
CONV_WIDTH = 31
CONV_HALF = CONV_WIDTH // 2
N_HEADS = 8
N_KV_HEADS = 2
KV_GROUP = N_HEADS // N_KV_HEADS
C_WIDTH = N_HEADS * HEAD_DIM
KV_WIDTH = N_KV_HEADS * HEAD_DIM
GROUP_Q_WIDTH = KV_GROUP * HEAD_DIM
GRID_W = 64
ROPE_FREQS = 32
ROPE_THETA = 10000.0
ATTN_SCALE = HEAD_DIM ** -0.5
LOG2E = math.log2(math.e)
N_MOD = 6
LN_EPS = 1e-6
RMS_EPS = 1e-6
AB_OFF = 0
B_OFF = 2 * A_WIDTH
Q_OFF = B_OFF + 2 * B_WIDTH
K_OFF = Q_OFF + C_WIDTH
IN_COLS = K_OFF + 2 * KV_WIDTH
UNSHIFTED_SOFTMAX_MAX_LOG2 = 60.0

SUBLANES_BF16 = 16
MOD_ROWS = 8
MOD_TN = 512
INPROJ_TM = 256
CONV_HALO = 16
CONV_ROWS = 64
ATTN_TQ = 256
ATTN_TK = 1024
CONV_TILE = ATTN_TQ // N_KV_HEADS
OUT_TM = 512
FF_TM = 512
FF_TF = 1024
VMEM_LIMIT = 56 * 1024 * 1024

BF16 = jnp.bfloat16
F32 = jnp.float32


def _params(*sem):
    return pltpu.CompilerParams(dimension_semantics=sem, vmem_limit_bytes=VMEM_LIMIT)


def _ln(x):
    mu = jnp.mean(x, axis=-1, keepdims=True)
    xc = x - mu
    var = jnp.mean(xc * xc, axis=-1, keepdims=True)
    return xc * lax.rsqrt(var + LN_EPS)


def _split_row_specs(tm, d, n_lat_tiles, ctx_tile0):
    lat = pl.BlockSpec((tm, d), lambda i, *_: (jnp.minimum(i, n_lat_tiles - 1), 0))
    ctx = pl.BlockSpec((tm, d), lambda i, *_: (ctx_tile0 + jnp.maximum(i - n_lat_tiles, 0), 0))
    return lat, ctx


def _row_mod_idx(tm, n_lat_rows, seq_len):
    n_lat_tiles = n_lat_rows // tm
    tiles_per_seq = seq_len // tm
    n_batch = n_lat_rows // seq_len
    return lambda i, *_: (jnp.where(i < n_lat_tiles, i // tiles_per_seq, n_batch), 0, 0)


def _mod_kernel(cond_ref, w_ref, b_ref, o_ref):
    a = jax.nn.silu(cond_ref[...]).astype(BF16)
    o_ref[...] = jnp.dot(a, w_ref[...].astype(BF16), preferred_element_type=F32) + b_ref[...]


def _modulation(cond, w_mod, b_mod):
    depth, d, n = w_mod.shape
    return pl.pallas_call(
        _mod_kernel,
        grid=(depth, n // MOD_TN),
        in_specs=[
            pl.BlockSpec((MOD_ROWS, d), lambda l, j: (0, 0)),
            pl.BlockSpec((None, d, MOD_TN), lambda l, j: (l, 0, j)),
            pl.BlockSpec((None, 1, MOD_TN), lambda l, j: (l, 0, j)),
        ],
        out_specs=pl.BlockSpec((None, MOD_ROWS, MOD_TN), lambda l, j: (l, 0, j)),
        out_shape=jax.ShapeDtypeStruct((depth, MOD_ROWS, n), F32),
        compiler_params=_params("arbitrary", "arbitrary"),
        name="modulation",
    )(cond, w_mod, b_mod.reshape(depth, 1, n))


def _rms_rope(xh, gain, cos, sin_lo, sin_hi):
    ms = jnp.mean(xh * xh, axis=-1, keepdims=True)
    y = xh * lax.rsqrt(ms + RMS_EPS) * gain
    return y * cos + pltpu.roll(y, HEAD_DIM - ROPE_FREQS, 1) * sin_lo + pltpu.roll(y, ROPE_FREQS, 1) * sin_hi


def _inproj_kernel(xl_ref, xc_ref, m_ref, w_ref, alng_ref, alnb_ref, ws_ref, bs_ref, qg_ref, kg_ref, rope_ref,
                   oa_ref, z_ref, q_ref, k_ref, v_ref, *, n_lat_tiles):
    tm = xl_ref.shape[0]
    x = jnp.where(pl.program_id(0) < n_lat_tiles, xl_ref[...], xc_ref[...])
    h = _ln(x) * (1.0 + m_ref[1:2, :]) + m_ref[0:1, :]
    hb = h.astype(BF16)

    pa = jnp.dot(hb, w_ref[:, AB_OFF:B_OFF], preferred_element_type=F32)
    u = jax.nn.gelu(pa[:, :A_WIDTH])
    vn = (_ln(jax.nn.gelu(pa[:, A_WIDTH:])) * alng_ref[...] + alnb_ref[...]).astype(BF16)
    for c in range(tm // CHUNK):
        rows = slice(c * CHUNK, (c + 1) * CHUNK)
        for hh in range(A_HEADS):
            cols = slice(hh * HEAD_DIM, (hh + 1) * HEAD_DIM)
            mixed = jnp.dot(ws_ref[hh], vn[rows, cols], preferred_element_type=F32) + bs_ref[:, cols]
            oa_ref[rows, cols] = (u[rows, cols] * mixed).astype(oa_ref.dtype)

    pb = jnp.dot(hb, w_ref[:, B_OFF:Q_OFF], preferred_element_type=F32)
    z_ref[...] = pb[:, :B_WIDTH] * jax.nn.sigmoid(pb[:, B_WIDTH:])

    cos, sin_lo, sin_hi = rope_ref[0], rope_ref[1], rope_ref[2]
    pq = jnp.dot(hb, w_ref[:, Q_OFF:K_OFF], preferred_element_type=F32)
    q_gain = qg_ref[...] * (ATTN_SCALE * LOG2E)
    for hh in range(N_HEADS):
        cols = slice(hh * HEAD_DIM, (hh + 1) * HEAD_DIM)
        q_ref[:, cols] = _rms_rope(pq[:, cols], q_gain, cos, sin_lo, sin_hi).astype(q_ref.dtype)
    pkv = jnp.dot(hb, w_ref[:, K_OFF:IN_COLS], preferred_element_type=F32)
    for hh in range(N_KV_HEADS):
        cols = slice(hh * HEAD_DIM, (hh + 1) * HEAD_DIM)
        k_ref[:, cols] = _rms_rope(pkv[:, cols], kg_ref[...], cos, sin_lo, sin_hi).astype(k_ref.dtype)
    v_ref[...] = pkv[:, KV_WIDTH:].astype(v_ref.dtype)


def _inproj(x_lat, x_ctx, ctx_tile0, m, w_in, aln_g, aln_b, ws, bs_full, q_gain, k_gain, rope, *,
            n_rows, n_lat_rows, seq_len):
    d = x_lat.shape[1]
    tm = INPROJ_TM
    n_lat_tiles = n_lat_rows // tm
    tiles_per_seq = seq_len // tm

    def rope_idx(i):
        return (0, jnp.where(i < n_lat_tiles, i % tiles_per_seq, tiles_per_seq), 0)

    row = lambda i: (i, 0)
    const2 = lambda i: (0, 0)
    lat_spec, ctx_spec = _split_row_specs(tm, d, n_lat_tiles, ctx_tile0)
    return pl.pallas_call(
        functools.partial(_inproj_kernel, n_lat_tiles=n_lat_tiles),
        grid=(n_rows // tm,),
        in_specs=[
            lat_spec,
            ctx_spec,
            pl.BlockSpec((None, N_MOD, d), _row_mod_idx(tm, n_lat_rows, seq_len)),
            pl.BlockSpec((d, IN_COLS), const2, pipeline_mode=pl.Buffered(1)),
            pl.BlockSpec((1, A_WIDTH), const2),
            pl.BlockSpec((1, A_WIDTH), const2),
            pl.BlockSpec((A_HEADS, CHUNK, CHUNK), lambda i: (0, 0, 0)),
            pl.BlockSpec((CHUNK, A_WIDTH), const2),
            pl.BlockSpec((1, HEAD_DIM), const2),
            pl.BlockSpec((1, HEAD_DIM), const2),
            pl.BlockSpec((3, tm, HEAD_DIM), rope_idx),
        ],
        out_specs=[
            pl.BlockSpec((tm, A_WIDTH), row),
            pl.BlockSpec((tm, B_WIDTH), row),
            pl.BlockSpec((tm, C_WIDTH), row),
            pl.BlockSpec((tm, KV_WIDTH), row),
            pl.BlockSpec((tm, KV_WIDTH), row),
        ],
        out_shape=[
            jax.ShapeDtypeStruct((n_rows, A_WIDTH), BF16),
            jax.ShapeDtypeStruct((n_rows, B_WIDTH), F32),
            jax.ShapeDtypeStruct((n_rows, C_WIDTH), BF16),
            jax.ShapeDtypeStruct((n_rows, KV_WIDTH), BF16),
            jax.ShapeDtypeStruct((n_rows, KV_WIDTH), BF16),
        ],
        compiler_params=_params("arbitrary"),
        name="inproj",
    )(x_lat, x_ctx, m, w_in, aln_g, aln_b, ws, bs_full, q_gain, k_gain, rope)


def _conv_rows(zp_ref, zc_ref, zn_ref, w_ref, b_ref, g_ref, beta_ref, ext_ref, acc_ref, pos, last):
    t = zc_ref.shape[0]
    ext_ref[0:CONV_HALO, :] = jnp.where(pos > 0, zp_ref[...], 0.0)
    ext_ref[CONV_HALO:CONV_HALO + t, :] = zc_ref[...]
    ext_ref[CONV_HALO + t:, :] = jnp.where(pos < last, zn_ref[...], 0.0)
    first = CONV_HALO - CONV_HALF
    n_win = CONV_ROWS + 2 * CONV_HALO
    for base in range(0, t, CONV_ROWS):
        for lane0 in range(0, B_WIDTH, HEAD_DIM):
            lanes = slice(lane0, lane0 + HEAD_DIM)
            win = ext_ref[base:base + n_win, lanes]
            acc = jnp.broadcast_to(b_ref[:, lanes], (CONV_ROWS, HEAD_DIM))
            for r in range(8):
                taps = [j for j in range(CONV_WIDTH) if (first + j) % 8 == r]
                if not taps:
                    continue
                shifted = win if r == 0 else pltpu.roll(win, n_win - r, 0)
                for j in taps:
                    q = (first + j) // 8 * 8
                    acc = acc + w_ref[j:j + 1, lanes] * shifted[q:q + CONV_ROWS, :]
            acc_ref[base:base + CONV_ROWS, lanes] = acc
    return jax.nn.silu(_ln(acc_ref[...]) * g_ref[...] + beta_ref[...])


def _attn_kernel(unshifted_ref, q_ref, kl_ref, vl_ref, kc_ref, vc_ref, zp_ref, zc_ref, zn_ref, cw_ref, cb_ref,
                 cg_ref, cbeta_ref, *rest, nq_lat, n_kv_tiles, n_lat_conv, lat_conv_per_seq, ctx_conv_per_seq):
    n_cast = (len(rest) - 4) // 2
    o_ref, ob_ref, ext_ref, acc_ref = rest[n_cast], rest[n_cast + 1], rest[-2], rest[-1]
    for src_ref, dst_ref in zip(rest[:n_cast], rest[n_cast + 2:-2]):
        dst_ref[...] = src_ref[...].astype(dst_ref.dtype)

    def conv_group():
        c = (pl.program_id(0) * N_KV_HEADS + pl.program_id(1)) * pl.num_programs(2) + pl.program_id(2)
        lat = c < n_lat_conv
        pos = jnp.where(lat, c % lat_conv_per_seq, (c - n_lat_conv) % ctx_conv_per_seq)
        last = jnp.where(lat, lat_conv_per_seq - 1, ctx_conv_per_seq - 1)
        ob = _conv_rows(zp_ref, zc_ref, zn_ref, cw_ref, cb_ref, cg_ref, cbeta_ref, ext_ref, acc_ref, pos, last)
        ob_ref[...] = ob.astype(ob_ref.dtype)

    tq = q_ref.shape[0]
    rows = KV_GROUP * tq
    q = jnp.concatenate([q_ref[:, g * HEAD_DIM:(g + 1) * HEAD_DIM] for g in range(KV_GROUP)], axis=0)
    unshifted = unshifted_ref[0] != 0
    is_latent = pl.program_id(2) < nq_lat

    def transposed_tile(k, v):
        st = lax.dot_general(k, q, (((1,), (1,)), ((), ())), preferred_element_type=F32)
        pt = jnp.exp2(st)
        l8 = jnp.sum(pt.reshape(pt.shape[0] // 8, 8, rows), axis=0)
        ot = lax.dot_general(v, pt.astype(BF16), (((0,), (0,)), ((), ())), preferred_element_type=F32)
        return ot, l8

    def write_transposed(ot, l8):
        out_t = ot / jnp.sum(l8, axis=0, keepdims=True)
        for g in range(KV_GROUP):
            o_ref[:, g * HEAD_DIM:(g + 1) * HEAD_DIM] = out_t[:, g * tq:(g + 1) * tq].T.astype(o_ref.dtype)

    @pl.when(jnp.logical_and(unshifted, is_latent))
    def _():
        conv_group()
        ot, l8 = transposed_tile(kc_ref[...], vc_ref[...])
        for j in range(n_kv_tiles):
            tile = slice(j * ATTN_TK, (j + 1) * ATTN_TK)
            ot_j, l8_j = transposed_tile(kl_ref[tile, :], vl_ref[tile, :])
            ot, l8 = ot + ot_j, l8 + l8_j
        write_transposed(ot, l8)

    @pl.when(jnp.logical_and(unshifted, jnp.logical_not(is_latent)))
    def _():
        conv_group()
        write_transposed(*transposed_tile(kc_ref[...], vc_ref[...]))

    @pl.when(jnp.logical_not(unshifted))
    def _():
        conv_group()

        def step(carry, k, v):
            m, l, acc = carry
            s = lax.dot_general(q, k, (((1,), (1,)), ((), ())), preferred_element_type=F32)
            m_new = jnp.maximum(m, jnp.max(s, axis=-1, keepdims=True))
            alpha = jnp.exp2(m - m_new)
            p = jnp.exp2(s - m_new)
            l_new = alpha * l + jnp.sum(p, axis=-1, keepdims=True)
            acc_new = alpha * acc + jnp.dot(p.astype(BF16), v, preferred_element_type=F32)
            return m_new, l_new, acc_new

        def latent_step(j, carry):
            off = pl.multiple_of(j * ATTN_TK, ATTN_TK)
            return step(carry, kl_ref[pl.ds(off, ATTN_TK), :], vl_ref[pl.ds(off, ATTN_TK), :])

        init = (jnp.full((rows, 1), -jnp.inf, F32), jnp.zeros((rows, 1), F32), jnp.zeros((rows, HEAD_DIM), F32))
        carry = step(init, kc_ref[...], vc_ref[...])
        m, l, acc = lax.fori_loop(0, jnp.where(is_latent, n_kv_tiles, 0), latent_step, carry)
        out = acc / l
        for g in range(KV_GROUP):
            o_ref[:, g * HEAD_DIM:(g + 1) * HEAD_DIM] = out[g * tq:(g + 1) * tq, :].astype(o_ref.dtype)


def _attention(unshifted, q, k, v, z, w_dw, b_dw, cln_g, cln_b, cast_weights, *, n_batch, seq_len, ctx_len,
               with_ctx_queries):
    tq = ATTN_TQ
    nq_lat = seq_len // tq
    nq_ctx = ctx_len // tq if with_ctx_queries else 0
    nq = nq_lat + nq_ctx
    ctx_block0 = (n_batch * seq_len) // ctx_len
    n_steps = n_batch * N_KV_HEADS * nq
    n_out_rows = n_steps * CONV_TILE
    n_slabs = n_batch * N_KV_HEADS * nq_lat
    halo_per_tile = CONV_TILE // CONV_HALO
    n_halo_blocks = z.shape[0] // CONV_HALO

    def q_idx(b, kv, qi):
        lat = b * nq_lat + qi
        ctx = n_batch * nq_lat + b * (ctx_len // tq) + (qi - nq_lat)
        return (jnp.where(qi < nq_lat, lat, ctx), kv)

    step = lambda b, kv, qi: (b * N_KV_HEADS + kv) * nq + qi
    slab = lambda b, kv, qi: jnp.minimum(step(b, kv, qi), n_slabs - 1)
    lat_idx = lambda b, kv, qi: (b, kv)
    ctx_idx = lambda b, kv, qi: (ctx_block0 + b, kv)
    const2 = lambda b, kv, qi: (0, 0)
    cast_in_specs, cast_out_specs, cast_out_shapes = [], [], []
    for w, layer in cast_weights:
        _, r, c = w.shape
        assert r % (n_slabs * SUBLANES_BF16) == 0
        cast_in_specs.append(
            pl.BlockSpec((None, r // n_slabs, c), lambda b, kv, qi, layer=layer: (layer, slab(b, kv, qi), 0)))
        cast_out_specs.append(pl.BlockSpec((r // n_slabs, c), lambda b, kv, qi: (slab(b, kv, qi), 0)))
        cast_out_shapes.append(jax.ShapeDtypeStruct((r, c), BF16))
    kern = functools.partial(
        _attn_kernel, nq_lat=nq_lat, n_kv_tiles=seq_len // ATTN_TK, n_lat_conv=(n_batch * seq_len) // CONV_TILE,
        lat_conv_per_seq=seq_len // CONV_TILE, ctx_conv_per_seq=ctx_len // CONV_TILE)
    outs = pl.pallas_call(
        kern,
        grid=(n_batch, N_KV_HEADS, nq),
        in_specs=[
            pl.BlockSpec(memory_space=pltpu.SMEM),
            pl.BlockSpec((tq, GROUP_Q_WIDTH), q_idx),
            pl.BlockSpec((seq_len, HEAD_DIM), lat_idx),
            pl.BlockSpec((seq_len, HEAD_DIM), lat_idx),
            pl.BlockSpec((ctx_len, HEAD_DIM), ctx_idx),
            pl.BlockSpec((ctx_len, HEAD_DIM), ctx_idx),
            pl.BlockSpec((CONV_HALO, B_WIDTH),
                         lambda b, kv, qi: (jnp.maximum(step(b, kv, qi) * halo_per_tile - 1, 0), 0)),
            pl.BlockSpec((CONV_TILE, B_WIDTH), lambda b, kv, qi: (step(b, kv, qi), 0)),
            pl.BlockSpec((CONV_HALO, B_WIDTH),
                         lambda b, kv, qi: (jnp.minimum((step(b, kv, qi) + 1) * halo_per_tile, n_halo_blocks - 1), 0)),
            pl.BlockSpec((CONV_WIDTH, B_WIDTH), const2),
            pl.BlockSpec((1, B_WIDTH), const2),
            pl.BlockSpec((1, B_WIDTH), const2),
            pl.BlockSpec((1, B_WIDTH), const2),
        ] + cast_in_specs,
        out_specs=[pl.BlockSpec((tq, GROUP_Q_WIDTH), q_idx),
                   pl.BlockSpec((CONV_TILE, B_WIDTH), lambda b, kv, qi: (step(b, kv, qi), 0))] + cast_out_specs,
        out_shape=[jax.ShapeDtypeStruct((n_out_rows, C_WIDTH), BF16),
                   jax.ShapeDtypeStruct((n_out_rows, B_WIDTH), BF16)] + cast_out_shapes,
        scratch_shapes=[pltpu.VMEM((CONV_TILE + 2 * CONV_HALO, B_WIDTH), F32), pltpu.VMEM((CONV_TILE, B_WIDTH), F32)],
        compiler_params=_params("arbitrary", "arbitrary", "arbitrary"),
        name="attention",
    )(unshifted, q, k, v, k, v, z, z, z, w_dw, b_dw, cln_g, cln_b, *[w for w, _ in cast_weights])
    return outs[0], outs[1], outs[2:]


def _outproj_kernel(oa_ref, ob_ref, oc_ref, w_ref, xl_ref, xc_ref, m_ref, g_ref, b_ref, o_ref, mix_ref, *,
                    alpha, n_lat_tiles):
    mix_ref[:, :A_WIDTH] = oa_ref[...]
    mix_ref[:, A_WIDTH:A_WIDTH + B_WIDTH] = ob_ref[...]
    mix_ref[:, A_WIDTH + B_WIDTH:] = oc_ref[...]
    y = jnp.dot(mix_ref[...], w_ref[...], preferred_element_type=F32)
    x = jnp.where(pl.program_id(0) < n_lat_tiles, xl_ref[...], xc_ref[...])
    r = alpha * x + m_ref[2:3, :] * y
    o_ref[...] = _ln(r) * g_ref[...] + b_ref[...]


def _outproj(oa, ob, oc, w_out, x_lat, x_ctx, ctx_tile0, m, ln_g, ln_b, *, n_rows, n_lat_rows, seq_len, alpha):
    d = x_lat.shape[1]
    tm = OUT_TM
    n_lat_tiles = n_lat_rows // tm
    row = lambda i: (i, 0)
    const2 = lambda i: (0, 0)
    lat_spec, ctx_spec = _split_row_specs(tm, d, n_lat_tiles, ctx_tile0)
    return pl.pallas_call(
        functools.partial(_outproj_kernel, alpha=alpha, n_lat_tiles=n_lat_tiles),
        grid=(n_rows // tm,),
        in_specs=[
            pl.BlockSpec((tm, A_WIDTH), row),
            pl.BlockSpec((tm, B_WIDTH), row),
            pl.BlockSpec((tm, C_WIDTH), row),
            pl.BlockSpec(w_out.shape, const2, pipeline_mode=pl.Buffered(1)),
            lat_spec,
            ctx_spec,
            pl.BlockSpec((None, N_MOD, d), _row_mod_idx(tm, n_lat_rows, seq_len)),
            pl.BlockSpec((1, d), const2),
            pl.BlockSpec((1, d), const2),
        ],
        out_specs=pl.BlockSpec((tm, d), row),
        out_shape=jax.ShapeDtypeStruct((n_rows, d), F32),
        scratch_shapes=[pltpu.VMEM((tm, w_out.shape[0]), BF16)],
        compiler_params=_params("arbitrary"),
        name="outproj",
    )(oa, ob, oc, w_out, x_lat, x_ctx, m, ln_g, ln_b)


def _ff_kernel(x_ref, m_ref, w1_ref, w2_ref, g_ref, b_ref, o_ref, xm_ref, acc_ref, *, alpha):
    j = pl.program_id(1)

    @pl.when(j == 0)
    def _():
        xm_ref[...] = (_ln(x_ref[...]) * (1.0 + m_ref[4:5, :]) + m_ref[3:4, :]).astype(BF16)
        acc_ref[...] = jnp.zeros_like(acc_ref)

    h = jnp.dot(xm_ref[...], w1_ref[...], preferred_element_type=F32)
    a = jnp.square(jnp.maximum(h, 0.0)).astype(BF16)
    acc_ref[...] += jnp.dot(a, w2_ref[...], preferred_element_type=F32)

    @pl.when(j == pl.num_programs(1) - 1)
    def _():
        r = alpha * x_ref[...] + m_ref[5:6, :] * acc_ref[...]
        o_ref[...] = _ln(r) * g_ref[...] + b_ref[...]


def _ff(xa, m, w1, w2, ln_g, ln_b, *, n_rows, n_lat_rows, seq_len, alpha):
    d = xa.shape[1]
    d_ff = w1.shape[1]
    tm, tf = FF_TM, FF_TF
    row = lambda i, j: (i, 0)
    const2 = lambda i, j: (0, 0)
    return pl.pallas_call(
        functools.partial(_ff_kernel, alpha=alpha),
        grid=(n_rows // tm, d_ff // tf),
        in_specs=[
            pl.BlockSpec((tm, d), row),
            pl.BlockSpec((None, N_MOD, d), _row_mod_idx(tm, n_lat_rows, seq_len)),
            pl.BlockSpec((d, tf), lambda i, j: (0, j)),
            pl.BlockSpec((tf, d), lambda i, j: (j, 0)),
            pl.BlockSpec((1, d), const2),
            pl.BlockSpec((1, d), const2),
        ],
        out_specs=pl.BlockSpec((tm, d), row),
        out_shape=jax.ShapeDtypeStruct((n_rows, d), F32),
        scratch_shapes=[pltpu.VMEM((tm, d), BF16), pltpu.VMEM((tm, d), F32)],
        compiler_params=_params("arbitrary", "arbitrary"),
        name="ff",
    )(xa, m, w1, w2, ln_g, ln_b)


def _rope_table(seq_len, tile):
    t = jnp.arange(seq_len, dtype=jnp.int32)
    row = (t // GRID_W).astype(F32)
    col = (t % GRID_W).astype(F32)
    freqs = ROPE_THETA ** (-jnp.arange(ROPE_FREQS, dtype=F32) / ROPE_FREQS)
    ar, ac = row[:, None] * freqs, col[:, None] * freqs
    zeros = jnp.zeros_like(ar)
    cos = jnp.concatenate([jnp.cos(ar), jnp.cos(ar), jnp.cos(ac), jnp.cos(ac)], axis=1)
    sin_lo = jnp.concatenate([-jnp.sin(ar), zeros, -jnp.sin(ac), zeros], axis=1)
    sin_hi = jnp.concatenate([zeros, jnp.sin(ar), zeros, jnp.sin(ac)], axis=1)
    table = jnp.stack([cos, sin_lo, sin_hi])
    ident = jnp.stack([jnp.ones((tile, HEAD_DIM), F32), jnp.zeros((tile, HEAD_DIM), F32),
                       jnp.zeros((tile, HEAD_DIM), F32)])
    return jnp.concatenate([table, ident], axis=1)


def kernel(x, c, ctx, c_ctx, w_mod, b_mod, w_in, a_ln_g, a_ln_b, a_ws, a_bs, b_dw, b_dw_bias, b_ln_g, b_ln_b,
           q_gain, k_gain, w_out, ln1_g, ln1_b, w_ff1, w_ff2, ln2_g, ln2_b):
    n_batch, seq_len, d = x.shape
    ctx_len = ctx.shape[1]
    depth = w_mod.shape[0]
    assert w_in.shape[2] == IN_COLS and w_out.shape[1] == A_WIDTH + B_WIDTH + C_WIDTH
    assert seq_len % ATTN_TK == 0 and seq_len % FF_TM == 0 and ctx_len % ATTN_TQ == 0
    assert seq_len % INPROJ_TM == 0 and ctx_len % INPROJ_TM == 0
    assert (n_batch * ctx_len) % FF_TM == 0 and n_batch + 1 <= MOD_ROWS
    alpha = float((2 * depth) ** 0.25)
    n_lat = n_batch * seq_len
    n_ctx = n_batch * ctx_len
    n_all = n_lat + n_ctx

    cond = jnp.zeros((MOD_ROWS, d), F32).at[:n_batch].set(c).at[n_batch].set(c_ctx)
    m_all = _modulation(cond, w_mod, b_mod).reshape(depth, MOD_ROWS, N_MOD, d)
    rope = _rope_table(seq_len, INPROJ_TM)
    ws_b = a_ws.astype(BF16)
    w_in_b = w_in[0].astype(BF16)
    score_bound = HEAD_DIM * ATTN_SCALE * LOG2E * jnp.max(jnp.abs(q_gain), axis=1) * jnp.max(jnp.abs(k_gain), axis=1)
    unshifted = (score_bound <= UNSHIFTED_SOFTMAX_MAX_LOG2).astype(jnp.int32)
    geom = dict(n_lat_rows=n_lat, seq_len=seq_len)

    x_lat, x_ctx, ctx_row0 = x.reshape(n_lat, d), ctx.reshape(n_ctx, d), 0
    for l in range(depth):
        last = l == depth - 1
        n_rows = n_lat if last else n_all
        m = m_all[l]
        bs_full = jnp.repeat(a_bs[l].T, HEAD_DIM, axis=1)
        oa, z, q, k, v = _inproj(
            x_lat, x_ctx, ctx_row0 // INPROJ_TM, m, w_in_b, a_ln_g[l][None], a_ln_b[l][None], ws_b[l], bs_full,
            q_gain[l][None], k_gain[l][None], rope, n_rows=n_all, **geom)
        to_cast = [(w_out, l), (w_ff1, l), (w_ff2, l)] + ([] if last else [(w_in, l + 1)])
        oc, ob, cast = _attention(
            unshifted[l:l + 1], q, k, v, z, b_dw[l], b_dw_bias[l][None], b_ln_g[l][None], b_ln_b[l][None], to_cast,
            n_batch=n_batch, seq_len=seq_len, ctx_len=ctx_len, with_ctx_queries=not last)
        w_out_b, w1_b, w2_b = cast[:3]
        x1 = _outproj(oa, ob, oc, w_out_b, x_lat, x_ctx, ctx_row0 // OUT_TM, m, ln1_g[l][None], ln1_b[l][None],
                      n_rows=n_rows, alpha=alpha, **geom)
        xa = _ff(x1, m, w1_b, w2_b, ln2_g[l][None], ln2_b[l][None], n_rows=n_rows, alpha=alpha, **geom)
        x_lat, x_ctx, ctx_row0 = xa, xa, n_lat
        if not last:
            w_in_b = cast[3]
    return xa.reshape(n_batch, seq_len, d)
```
